```python
import jax, jax.numpy as jnp
from jax import lax
import numpy as np

D_MODEL = 4096
BATCH = 1
SEQ = 8192
DEPTH = 1

MIX_WIDTH = D_MODEL
GDN_WIDTH = MIX_WIDTH // 2
SC_WIDTH = MIX_WIDTH - GDN_WIDTH
GDN_HEAD_DIM = 128
GDN_HEADS = GDN_WIDTH // GDN_HEAD_DIM
GDN_CONV = 4
GDN_CHUNK = 64
SC_GROUPS = 16
SC_CONV = 3
N_EXPERTS = 32
TOP_K = 4
D_EXPERT = D_MODEL // 2
SWIGLU_LIMIT = 7.0
SWIGLU_ALPHA = 1.702
MOE_BLOCK = 256
NORM_EPS = 1e-6
IN_SPLITS = (GDN_WIDTH, GDN_WIDTH, GDN_WIDTH, GDN_WIDTH, GDN_HEADS, GDN_HEADS, SC_WIDTH, SC_WIDTH, SC_WIDTH)
IN_WIDTH = sum(IN_SPLITS)

kernel_name = "hybrid_gdn_shortconv_moe_block"


def rmsnorm(x, w):
    xf = x.astype(jnp.float32)
    y = xf * lax.rsqrt(jnp.mean(xf * xf, axis=-1, keepdims=True) + NORM_EPS)
    return (y * w.astype(jnp.float32)).astype(x.dtype)


def l2norm(t):
    return t * lax.rsqrt(jnp.sum(t * t, axis=-1, keepdims=True) + NORM_EPS)


def causal_depthwise_conv(u, w):
    K = w.shape[0]
    S = u.shape[1]
    up = jnp.pad(u, ((0, 0), (K - 1, 0), (0, 0)))
    out = w[0] * up[:, 0:S]
    for i in range(1, K):
        out = out + w[i] * up[:, i:i + S]
    return out


def chunked_gated_delta_rule(q, k, v, g, beta):
    Bsz, S, H, dk = q.shape
    dv = v.shape[-1]
    C = GDN_CHUNK
    N = S // C

    def chunks(t):
        t = t.reshape((Bsz, N, C, H) + t.shape[3:])
        return jnp.moveaxis(t, (1, 3), (0, 2))

    q = chunks(q * (dk ** -0.5))
    k = chunks(k)
    v = chunks(v)
    beta = chunks(beta)
    g = jnp.cumsum(chunks(g), axis=-1)
    causal = jnp.tril(jnp.ones((C, C), dtype=bool))
    strict = jnp.tril(jnp.ones((C, C), dtype=bool), -1)
    decay = jnp.exp(jnp.where(causal, g[..., :, None] - g[..., None, :], -jnp.inf))
    kb = k * beta[..., None]
    L = jnp.where(strict, jnp.einsum('nbhid,nbhjd->nbhij', kb, k) * decay, 0.0)
    A = L + jnp.eye(C, dtype=L.dtype)
    rhs = jnp.concatenate([v * beta[..., None], kb * jnp.exp(g)[..., None]], axis=-1)
    uw = lax.linalg.triangular_solve(A, rhs, left_side=True, lower=True, unit_diagonal=True)
    u, w = uw[..., :dv], uw[..., dv:]
    qk = jnp.einsum('nbhid,nbhjd->nbhij', q, k) * decay
    q_dec = q * jnp.exp(g)[..., None]
    g_last = g[..., -1]
    k_dec = k * jnp.exp(g_last[..., None] - g)[..., None]

    def step(state, xs):
        q_n, qk_n, u_n, w_n, k_n, gl_n = xs
        v_new = u_n - jnp.einsum('bhck,bhkv->bhcv', w_n, state)
        o = jnp.einsum('bhck,bhkv->bhcv', q_n, state) + jnp.einsum('bhij,bhjv->bhiv', qk_n, v_new)
        state = state * jnp.exp(gl_n)[..., None, None] + jnp.einsum('bhck,bhcv->bhkv', k_n, v_new)
        return state, o

    state0 = jnp.zeros((Bsz, H, dk, dv), jnp.float32)
    _, o = lax.scan(step, state0, (q_dec, qk, u, w, k_dec, g_last))
    return jnp.moveaxis(o, (0, 2), (1, 3)).reshape(Bsz, S, H, dv)


def gdn_mixer(q, k, v, z, b_raw, a_raw, conv_w, a_log, dt_bias, norm_w):
    Bsz, S, _ = q.shape
    out_dtype = q.dtype
    qkv = jax.nn.silu(causal_depthwise_conv(jnp.concatenate([q, k, v], axis=-1), conv_w))
    q, k, v = jnp.split(qkv, 3, axis=-1)

    def heads(t):
        return t.reshape(Bsz, S, GDN_HEADS, GDN_HEAD_DIM).astype(jnp.float32)

    q = l2norm(heads(q))
    k = l2norm(heads(k))
    v = heads(v)
    beta = jax.nn.sigmoid(b_raw.astype(jnp.float32))
    g = -jnp.exp(a_log.astype(jnp.float32)) * jax.nn.softplus(a_raw.astype(jnp.float32) + dt_bias.astype(jnp.float32))
    o = chunked_gated_delta_rule(q, k, v, g, beta)
    o = o * lax.rsqrt(jnp.mean(o * o, axis=-1, keepdims=True) + NORM_EPS)
    o = o * norm_w.astype(jnp.float32) * jax.nn.silu(heads(z))
    return o.reshape(Bsz, S, GDN_WIDTH).astype(out_dtype)


def shortconv_mixer(b, c, h, conv_w):
    return b * causal_depthwise_conv(c * h, conv_w)


def moe_ffn(h, router_w, router_b, w_gate_up, b_gate_up, w_down, b_down):
    T, D = h.shape
    logits = (h @ router_w + router_b).astype(jnp.float32)
    top_logits, top_idx = lax.top_k(logits, TOP_K)
    gates = jax.nn.softmax(top_logits, axis=-1)
    n_assign = T * TOP_K
    flat_e = top_idx.reshape(n_assign).astype(jnp.int32)
    flat_tok = jnp.repeat(jnp.arange(T, dtype=jnp.int32), TOP_K)
    order = jnp.argsort(flat_e)
    sorted_e = flat_e[order]
    sorted_tok = flat_tok[order]
    sorted_gate = gates.reshape(n_assign)[order]
    counts = jnp.bincount(flat_e, length=N_EXPERTS)
    padded = (counts + MOE_BLOCK - 1) // MOE_BLOCK * MOE_BLOCK
    start = jnp.cumsum(counts) - counts
    pad_end = jnp.cumsum(padded)
    pad_start = pad_end - padded
    slot = pad_start[sorted_e] + (jnp.arange(n_assign, dtype=jnp.int32) - start[sorted_e])
    n_blocks = -(-(n_assign + N_EXPERTS * (MOE_BLOCK - 1)) // MOE_BLOCK)
    slot_tok = jnp.full((n_blocks * MOE_BLOCK,), T, jnp.int32).at[slot].set(sorted_tok)
    block_e = jnp.minimum(
        jnp.searchsorted(pad_end, jnp.arange(n_blocks, dtype=jnp.int32) * MOE_BLOCK, side='right'),
        N_EXPERTS - 1)
    h_pad = jnp.concatenate([h, jnp.zeros((1, D), h.dtype)], axis=0)

    def expert_block(args):
        tok, e = args
        xb = h_pad[tok]
        gu = xb @ w_gate_up[e] + b_gate_up[e]
        gate, up = gu[:, :D_EXPERT], gu[:, D_EXPERT:]
        gate = jnp.minimum(gate, SWIGLU_LIMIT)
        up = jnp.clip(up, -SWIGLU_LIMIT, SWIGLU_LIMIT)
        glu = gate * jax.nn.sigmoid(gate * SWIGLU_ALPHA)
        return ((up + 1.0) * glu) @ w_down[e] + b_down[e]

    out = lax.map(expert_block, (slot_tok.reshape(n_blocks, MOE_BLOCK), block_e)).reshape(-1, D)
    y = out[slot] * sorted_gate[:, None].astype(h.dtype)
    return jax.ops.segment_sum(y, sorted_tok, num_segments=T)


def setup_inputs(seed: int = 0) -> dict:
    key = jax.random.key(seed)
    ks = jax.random.split(key, 20)
    f32 = jnp.float32
    L = DEPTH

    def normal(k, shape, scale):
        return jax.random.normal(k, shape, f32) * scale

    return {
        "x": normal(ks[0], (BATCH, SEQ, D_MODEL), 1.0),
        "attn_norm_w": 1.0 + normal(ks[1], (L, D_MODEL), 0.01),
        "w_in": normal(ks[2], (L, D_MODEL, IN_WIDTH), D_MODEL ** -0.5),
        "gdn_conv_w": normal(ks[3], (L, GDN_CONV, 3 * GDN_WIDTH), GDN_CONV ** -0.5),
        "gdn_a_log": jnp.log(jax.random.uniform(ks[4], (L, GDN_HEADS), f32, 1.0, 16.0)),
        "gdn_dt_bias": normal(ks[5], (L, GDN_HEADS), 0.1),
        "gdn_norm_w": 1.0 + normal(ks[6], (L, GDN_HEAD_DIM), 0.01),
        "sc_conv_w": normal(ks[7], (L, SC_CONV, SC_WIDTH), SC_CONV ** -0.5),
        "w_out": normal(ks[8], (L, MIX_WIDTH, D_MODEL), MIX_WIDTH ** -0.5),
        "ffn_norm_w": 1.0 + normal(ks[9], (L, D_MODEL), 0.01),
        "router_w": normal(ks[10], (L, D_MODEL, N_EXPERTS), D_MODEL ** -0.5),
        "router_b": normal(ks[11], (L, N_EXPERTS), 0.01),
        "w_gate_up": normal(ks[12], (L, N_EXPERTS, D_MODEL, 2 * D_EXPERT), D_MODEL ** -0.5),
        "b_gate_up": normal(ks[13], (L, N_EXPERTS, 2 * D_EXPERT), 0.01),
        "w_down": normal(ks[14], (L, N_EXPERTS, D_EXPERT, D_MODEL), D_EXPERT ** -0.5),
        "b_down": normal(ks[15], (L, N_EXPERTS, D_MODEL), 0.01),
        "final_norm_w": 1.0 + normal(ks[16], (D_MODEL,), 0.01),
    }


def reference(x, attn_norm_w, w_in, gdn_conv_w, gdn_a_log, gdn_dt_bias, gdn_norm_w, sc_conv_w,
              w_out, ffn_norm_w, router_w, router_b, w_gate_up, b_gate_up, w_down, b_down,
              final_norm_w):
    Bsz, S, D = x.shape
    split_points = [int(p) for p in np.cumsum(np.array(IN_SPLITS))[:-1]]
    for l in range(DEPTH):
        h = rmsnorm(x, attn_norm_w[l])
        proj = h @ w_in[l]
        q, k, v, z, b_raw, a_raw, sc_b, sc_c, sc_h = jnp.split(proj, split_points, axis=-1)
        y_gdn = gdn_mixer(q, k, v, z, b_raw, a_raw, gdn_conv_w[l], gdn_a_log[l],
                          gdn_dt_bias[l], gdn_norm_w[l])
        y_sc = shortconv_mixer(sc_b, sc_c, sc_h, sc_conv_w[l])
        x = x + jnp.concatenate([y_gdn, y_sc], axis=-1) @ w_out[l]
        h = rmsnorm(x, ffn_norm_w[l])
        y = moe_ffn(h.reshape(Bsz * S, D), router_w[l], router_b[l], w_gate_up[l],
                    b_gate_up[l], w_down[l], b_down[l])
        x = x + y.reshape(Bsz, S, D).astype(x.dtype)
    return rmsnorm(x, final_norm_w)
```

```python
import functools

import jax
import jax.numpy as jnp
from jax import lax
from jax.experimental import pallas as pl
from jax.experimental.pallas import tpu as pltpu

NORM_EPS = 1e-6
TOP_K = 4
SWIGLU_LIMIT = 7.0
SWIGLU_ALPHA = 1.702
GDN_CONV = 4
SC_CONV = 3
MOE_BLOCK = 256
LANES = 128
SUBLANES = 8
HEAD_DIM = 128
CHUNK = 128
VMEM_LIMIT = 56 * 1024 * 1024

F32 = jnp.float32
BF16 = jnp.bfloat16
HIGHEST = lax.Precision.HIGHEST


def _params(n_axes):
    return pltpu.CompilerParams(dimension_semantics=("arbitrary",) * n_axes,
                                vmem_limit_bytes=VMEM_LIMIT)


def _sigmoid(x):
    return 1.0 / (1.0 + jnp.exp(-x))


def _mm(a, b):
    return jnp.dot(a.astype(BF16), b.astype(BF16), preferred_element_type=F32)


def _mm_f32(a, b):
    return jnp.dot(a, b, preferred_element_type=F32, precision=HIGHEST)


def _mm_nt(a, b):
    return lax.dot_general(a.astype(BF16), b.astype(BF16), (((1,), (1,)), ((), ())),
                           preferred_element_type=F32)


def _mm_tn(a, b):
    return lax.dot_general(a.astype(BF16), b.astype(BF16), (((0,), (0,)), ((), ())),
                           preferred_element_type=F32)


def _norm_in_kernel(x_ref, w_ref, wba_ref, h_ref, ba_ref):
    x = x_ref[...]
    h = x * lax.rsqrt(jnp.mean(x * x, axis=-1, keepdims=True) + NORM_EPS) * w_ref[...]
    hb = h.astype(BF16)
    h_ref[...] = hb
    ba_ref[...] = jnp.dot(hb, wba_ref[...], preferred_element_type=F32)


def _norm_in(x, w, wba):
    T, D = x.shape
    tm = min(256, T)
    return pl.pallas_call(
        _norm_in_kernel,
        grid=(T // tm,),
        in_specs=[pl.BlockSpec((tm, D), lambda i: (i, 0)),
                  pl.BlockSpec((1, D), lambda i: (0, 0)),
                  pl.BlockSpec((D, LANES), lambda i: (0, 0))],
        out_specs=[pl.BlockSpec((tm, D), lambda i: (i, 0)),
                   pl.BlockSpec((tm, LANES), lambda i: (i, 0))],
        out_shape=[jax.ShapeDtypeStruct((T, D), BF16),
                   jax.ShapeDtypeStruct((T, LANES), F32)],
        compiler_params=_params(1),
        name="norm_in",
    )(x, w, wba)


def _matmul_kernel(a_ref, b_ref, o_ref):
    o_ref[...] = jnp.dot(a_ref[...], b_ref[...], preferred_element_type=F32)


def _matmul(a, b, name):
    M, K = a.shape
    N = b.shape[1]
    tm, tn = min(1024, M), min(1024, N)
    return pl.pallas_call(
        _matmul_kernel,
        grid=(N // tn, M // tm),
        in_specs=[pl.BlockSpec((tm, K), lambda j, i: (i, 0)),
                  pl.BlockSpec((K, tn), lambda j, i: (0, j))],
        out_specs=pl.BlockSpec((tm, tn), lambda j, i: (i, j)),
        out_shape=jax.ShapeDtypeStruct((M, N), F32),
        compiler_params=_params(2),
        name=name,
    )(a, b)


def _gdn_kernel(q_ref, k_ref, v_ref, z_ref, ba_ref, wq_ref, wk_ref, wv_ref, par_ref, nw_ref,
                o_ref, qbuf, kbuf, vbuf, state, *, nheads, tt):
    h = pl.program_id(0)
    i = pl.program_id(1)
    C = CHUNK

    @pl.when(i == 0)
    def _():
        state[...] = jnp.zeros_like(state)
        zero_tail = jnp.zeros((SUBLANES, HEAD_DIM), F32)
        qbuf[0:SUBLANES, :] = zero_tail
        kbuf[0:SUBLANES, :] = zero_tail
        vbuf[0:SUBLANES, :] = zero_tail

    def conv_silu(x_ref, buf, w_ref):
        buf[SUBLANES:SUBLANES + tt, :] = x_ref[...]
        first = SUBLANES - (GDN_CONV - 1)
        acc = w_ref[0:1, :] * buf[first:first + tt, :]
        for j in range(1, GDN_CONV):
            acc = acc + w_ref[j:j + 1, :] * buf[first + j:first + j + tt, :]
        buf[0:SUBLANES, :] = buf[tt:tt + SUBLANES, :]
        return acc * _sigmoid(acc)

    def l2norm(t):
        return t * lax.rsqrt(jnp.sum(t * t, axis=-1, keepdims=True) + NORM_EPS)

    q = l2norm(conv_silu(q_ref, qbuf, wq_ref)) * (HEAD_DIM ** -0.5)
    k = l2norm(conv_silu(k_ref, kbuf, wk_ref))
    v = conv_silu(v_ref, vbuf, wv_ref)

    sel_row = lax.broadcasted_iota(jnp.int32, (LANES, LANES), 0)
    pick_b = (sel_row == h).astype(F32)
    pick_a = (sel_row == nheads + h).astype(F32)
    ba = ba_ref[...]
    b_raw = _mm_f32(ba, pick_b)
    a_raw = _mm_f32(ba, pick_a)
    par = _mm_f32(par_ref[...], pick_b)
    a_log = par[0:1, :]
    dt_bias = par[1:2, :]
    beta = _sigmoid(b_raw)
    sp_in = a_raw + dt_bias
    softplus = jnp.maximum(sp_in, 0.0) + jnp.log(1.0 + jnp.exp(-jnp.abs(sp_in)))
    g = -jnp.exp(a_log) * softplus

    r_i = lax.broadcasted_iota(jnp.int32, (C, C), 0)
    c_i = lax.broadcasted_iota(jnp.int32, (C, C), 1)
    causal = r_i >= c_i
    strict = r_i > c_i
    tril_incl = causal.astype(F32)
    eye = (r_i == c_i).astype(F32)

    S = state[...]
    nw = nw_ref[...]
    for c in range(tt // C):
        sl = slice(c * C, (c + 1) * C)
        qc, kc, vc, bc = q[sl], k[sl], v[sl], beta[sl]
        gc = _mm_f32(tril_incl, g[sl])
        diff = gc - gc.T
        decay = jnp.where(causal, jnp.exp(jnp.minimum(diff, 0.0)), 0.0)
        kb = kc * bc
        L = jnp.where(strict, _mm_nt(kb, kc) * decay, 0.0)
        X = -L
        P = eye + X
        for _ in range(C.bit_length() - 2):
            X = _mm_f32(X, X)
            P = P + _mm_f32(P, X)
        eg = jnp.exp(gc)
        uw = _mm_f32(P, jnp.concatenate([vc * bc, kb * eg], axis=-1))
        u, w = uw[:, :HEAD_DIM], uw[:, HEAD_DIM:]
        qk = _mm_nt(qc, kc) * decay
        g_last = gc[C - 1:C, :]
        k_dec = kc * jnp.exp(g_last - gc)
        v_new = u - _mm(w, S)
        o = _mm(qc * eg, S) + _mm(qk, v_new)
        S = S * jnp.exp(g_last) + _mm_tn(k_dec, v_new)
        o = o * lax.rsqrt(jnp.mean(o * o, axis=-1, keepdims=True) + NORM_EPS)
        zc = z_ref[sl, :]
        o_ref[sl, :] = (o * nw * (zc * _sigmoid(zc))).astype(o_ref.dtype)
    state[...] = S


def _gdn(qkvz, ba, conv_w, par, norm_w, nheads):
    T = qkvz.shape[0]
    tt = min(512, T // 2) if T >= 2 * CHUNK else T
    H = nheads
    col = lambda off: pl.BlockSpec((tt, HEAD_DIM), lambda h, i, off=off: (i, off + h))
    wcol = lambda off: pl.BlockSpec((GDN_CONV, HEAD_DIM), lambda h, i, off=off: (0, off + h))
    return pl.pallas_call(
        functools.partial(_gdn_kernel, nheads=H, tt=tt),
        grid=(H, T // tt),
        in_specs=[col(0), col(H), col(2 * H), col(3 * H),
                  pl.BlockSpec((tt, LANES), lambda h, i: (i, 0)),
                  wcol(0), wcol(H), wcol(2 * H),
                  pl.BlockSpec((SUBLANES, LANES), lambda h, i: (0, 0)),
                  pl.BlockSpec((1, HEAD_DIM), lambda h, i: (0, 0))],
        out_specs=pl.BlockSpec((tt, HEAD_DIM), lambda h, i: (i, h)),
        out_shape=jax.ShapeDtypeStruct((T, H * HEAD_DIM), BF16),
        scratch_shapes=[pltpu.VMEM((tt + SUBLANES, HEAD_DIM), F32)] * 3
                       + [pltpu.VMEM((HEAD_DIM, HEAD_DIM), F32)],
        compiler_params=_params(2),
        name="gdn",
    )(qkvz, qkvz, qkvz, qkvz, ba, conv_w, conv_w, conv_w, par, norm_w)


def _shortconv_kernel(b_ref, c_ref, h_ref, w_ref, o_ref, buf, *, ts):
    @pl.when(pl.program_id(1) == 0)
    def _():
        buf[0:SUBLANES, :] = jnp.zeros((SUBLANES, buf.shape[1]), F32)

    buf[SUBLANES:SUBLANES + ts, :] = c_ref[...] * h_ref[...]
    first = SUBLANES - (SC_CONV - 1)
    acc = w_ref[0:1, :] * buf[first:first + ts, :]
    for j in range(1, SC_CONV):
        acc = acc + w_ref[j:j + 1, :] * buf[first + j:first + j + ts, :]
    buf[0:SUBLANES, :] = buf[ts:ts + SUBLANES, :]
    o_ref[...] = (b_ref[...] * acc).astype(o_ref.dtype)


def _shortconv(proj_sc, conv_w):
    T = proj_sc.shape[0]
    W = conv_w.shape[1]
    ts = min(512, T // 2) if T >= 16 else T
    tc = min(512, W)
    nc = W // tc
    col = lambda off: pl.BlockSpec((ts, tc), lambda j, i, off=off: (i, off + j))
    return pl.pallas_call(
        functools.partial(_shortconv_kernel, ts=ts),
        grid=(nc, T // ts),
        in_specs=[col(0), col(nc), col(2 * nc),
                  pl.BlockSpec((SC_CONV, tc), lambda j, i: (0, j))],
        out_specs=pl.BlockSpec((ts, tc), lambda j, i: (i, j)),
        out_shape=jax.ShapeDtypeStruct((T, W), BF16),
        scratch_shapes=[pltpu.VMEM((ts + SUBLANES, tc), F32)],
        compiler_params=_params(2),
        name="shortconv",
    )(proj_sc, proj_sc, proj_sc, conv_w)


def _out_proj_kernel(x_ref, ya_ref, yb_ref, wa_ref, wb_ref, o_ref):
    o_ref[...] = (x_ref[...]
                  + jnp.dot(ya_ref[...], wa_ref[...], preferred_element_type=F32)
                  + jnp.dot(yb_ref[...], wb_ref[...], preferred_element_type=F32))


def _out_proj(x, ya, yb, w):
    T, D = x.shape
    Ka, Kb = ya.shape[1], yb.shape[1]
    assert Ka == Kb
    tm, tn = min(1024, T), min(1024, D)
    return pl.pallas_call(
        _out_proj_kernel,
        grid=(D // tn, T // tm),
        in_specs=[pl.BlockSpec((tm, tn), lambda j, i: (i, j)),
                  pl.BlockSpec((tm, Ka), lambda j, i: (i, 0)),
                  pl.BlockSpec((tm, Kb), lambda j, i: (i, 0)),
                  pl.BlockSpec((Ka, tn), lambda j, i: (0, j)),
                  pl.BlockSpec((Kb, tn), lambda j, i: (1, j))],
        out_specs=pl.BlockSpec((tm, tn), lambda j, i: (i, j)),
        out_shape=jax.ShapeDtypeStruct((T, D), F32),
        compiler_params=_params(2),
        name="out_proj",
    )(x, ya, yb, w, w)


def _router_kernel(x_ref, w_ref, rw_ref, rb_ref, h_ref, idx_ref, gate_ref, rank_ref, cnt_ref,
                   carry, *, n_experts, tm):
    @pl.when(pl.program_id(0) == 0)
    def _():
        carry[...] = jnp.zeros_like(carry)

    x = x_ref[...]
    h = x * lax.rsqrt(jnp.mean(x * x, axis=-1, keepdims=True) + NORM_EPS) * w_ref[...]
    h_ref[...] = h
    logits = _mm_f32(h, rw_ref[...]) + rb_ref[...]
    lane = lax.broadcasted_iota(jnp.int32, (tm, LANES), 1)
    lane_f = lane.astype(F32)
    neg_inf = jnp.float32(-jnp.inf)
    l = jnp.where(lane < n_experts, logits, neg_inf)
    tops, onehots, idxs = [], [], []
    for _ in range(TOP_K):
        m = jnp.max(l, axis=-1, keepdims=True)
        idx = jnp.min(jnp.where(l == m, lane_f, float(LANES)), axis=-1, keepdims=True)
        oh = lane_f == idx
        l = jnp.where(oh, neg_inf, l)
        tops.append(m)
        idxs.append(idx)
        onehots.append(oh)
    exps = [jnp.exp(m - tops[0]) for m in tops]
    denom = exps[0]
    for e in exps[1:]:
        denom = denom + e
    sel = jnp.zeros((tm, LANES), F32)
    for oh in onehots:
        sel = jnp.where(oh, 1.0, sel)
    r_i = lax.broadcasted_iota(jnp.int32, (tm, tm), 0)
    c_i = lax.broadcasted_iota(jnp.int32, (tm, tm), 1)
    cum = _mm((r_i > c_i).astype(F32), sel) + carry[0:1, :]
    idx_out = jnp.zeros((tm, LANES), F32)
    gate_out = jnp.zeros((tm, LANES), F32)
    rank_out = jnp.zeros((tm, LANES), F32)
    for r in range(TOP_K):
        rank = jnp.sum(jnp.where(onehots[r], cum, 0.0), axis=-1, keepdims=True)
        idx_out = jnp.where(lane == r, idxs[r], idx_out)
        gate_out = jnp.where(lane == r, exps[r] / denom, gate_out)
        rank_out = jnp.where(lane == r, rank, rank_out)
    idx_ref[...] = idx_out.astype(jnp.int32)
    gate_ref[...] = gate_out
    rank_ref[...] = rank_out.astype(jnp.int32)
    total = carry[...] + jnp.sum(sel, axis=0, keepdims=True)
    carry[...] = total
    cnt_ref[...] = total.astype(jnp.int32)


def _router(x1, w, rw, rb, n_experts):
    T, D = x1.shape
    tm = min(256, T)
    row = lambda: pl.BlockSpec((tm, LANES), lambda i: (i, 0))
    return pl.pallas_call(
        functools.partial(_router_kernel, n_experts=n_experts, tm=tm),
        grid=(T // tm,),
        in_specs=[pl.BlockSpec((tm, D), lambda i: (i, 0)),
                  pl.BlockSpec((1, D), lambda i: (0, 0)),
                  pl.BlockSpec((D, LANES), lambda i: (0, 0)),
                  pl.BlockSpec((1, LANES), lambda i: (0, 0))],
        out_specs=[pl.BlockSpec((tm, D), lambda i: (i, 0)), row(), row(), row(),
                   pl.BlockSpec((SUBLANES, LANES), lambda i: (0, 0))],
        out_shape=[jax.ShapeDtypeStruct((T, D), F32),
                   jax.ShapeDtypeStruct((T, LANES), jnp.int32),
                   jax.ShapeDtypeStruct((T, LANES), F32),
                   jax.ShapeDtypeStruct((T, LANES), jnp.int32),
                   jax.ShapeDtypeStruct((SUBLANES, LANES), jnp.int32)],
        scratch_shapes=[pltpu.VMEM((SUBLANES, LANES), F32)],
        compiler_params=_params(1),
        name="router",
    )(x1, w, rw, rb)


def _scatter_kernel(slots_ref, h_hbm, zeros_hbm, xs_hbm, sem, *, tc):
    del zeros_hbm
    base = pl.program_id(0) * tc

    def issue(t, carry):
        tok = base + t
        for j in range(TOP_K):
            s = slots_ref[tok * TOP_K + j]
            pltpu.make_async_copy(h_hbm.at[pl.ds(tok, 1)], xs_hbm.at[pl.ds(s, 1)], sem).start()
        return carry

    lax.fori_loop(0, tc, issue, 0)
    pltpu.make_async_copy(h_hbm.at[pl.ds(0, tc * TOP_K)], xs_hbm.at[pl.ds(0, tc * TOP_K)], sem).wait()


def _scatter(slots_flat, h2, n_rows):
    T, D = h2.shape
    tc = min(256, T // TOP_K)
    return pl.pallas_call(
        functools.partial(_scatter_kernel, tc=tc),
        grid_spec=pltpu.PrefetchScalarGridSpec(
            num_scalar_prefetch=1,
            grid=(T // tc,),
            in_specs=[pl.BlockSpec(memory_space=pl.ANY), pl.BlockSpec(memory_space=pl.ANY)],
            out_specs=pl.BlockSpec(memory_space=pl.ANY),
            scratch_shapes=[pltpu.SemaphoreType.DMA(())]),
        out_shape=jax.ShapeDtypeStruct((n_rows, D), h2.dtype),
        input_output_aliases={2: 0},
        compiler_params=_params(1),
        name="scatter",
    )(slots_flat, h2, jnp.zeros((n_rows, D), h2.dtype))


def _expert_changed(be_ref, b):
    return jnp.logical_or(b == 0, be_ref[b] != be_ref[jnp.maximum(b - 1, 0)])


def _expert_up_kernel(be_ref, nu_ref, x_ref, wg_ref, wu_ref, bg_ref, bu_ref, o_ref, wg_bf, wu_bf):
    b = pl.program_id(1)

    @pl.when(b >= nu_ref[0])
    def _():
        o_ref[...] = jnp.zeros_like(o_ref)

    @pl.when(b < nu_ref[0])
    def _():
        @pl.when(_expert_changed(be_ref, b))
        def _():
            wg_bf[...] = wg_ref[...].astype(BF16)
            wu_bf[...] = wu_ref[...].astype(BF16)

        x = x_ref[...].astype(BF16)
        gate = jnp.dot(x, wg_bf[...], preferred_element_type=F32) + bg_ref[...]
        up = jnp.dot(x, wu_bf[...], preferred_element_type=F32) + bu_ref[...]
        gate = jnp.minimum(gate, SWIGLU_LIMIT)
        up = jnp.clip(up, -SWIGLU_LIMIT, SWIGLU_LIMIT)
        glu = gate * _sigmoid(gate * SWIGLU_ALPHA)
        o_ref[...] = ((up + 1.0) * glu).astype(o_ref.dtype)


def _expert_up(block_e, n_used, xs, w_gate_up, b_gate_up):
    R, D = xs.shape
    E, _, two_de = w_gate_up.shape
    DE = two_de // 2
    tn = min(512, DE)
    nj = DE // tn
    nb = R // MOE_BLOCK

    def blk(b, nu):
        return jnp.minimum(b, nu[0] - 1)

    return pl.pallas_call(
        _expert_up_kernel,
        grid_spec=pltpu.PrefetchScalarGridSpec(
            num_scalar_prefetch=2,
            grid=(nj, nb),
            in_specs=[
                pl.BlockSpec((MOE_BLOCK, D), lambda j, b, be, nu: (blk(b, nu), 0)),
                pl.BlockSpec((None, D, tn), lambda j, b, be, nu: (be[blk(b, nu)], 0, j)),
                pl.BlockSpec((None, D, tn), lambda j, b, be, nu: (be[blk(b, nu)], 0, nj + j)),
                pl.BlockSpec((None, 1, tn), lambda j, b, be, nu: (be[blk(b, nu)], 0, j)),
                pl.BlockSpec((None, 1, tn), lambda j, b, be, nu: (be[blk(b, nu)], 0, nj + j)),
            ],
            out_specs=pl.BlockSpec((MOE_BLOCK, tn), lambda j, b, be, nu: (b, j)),
            scratch_shapes=[pltpu.VMEM((D, tn), BF16), pltpu.VMEM((D, tn), BF16)]),
        out_shape=jax.ShapeDtypeStruct((R, DE), BF16),
        compiler_params=_params(2),
        name="expert_up",
    )(block_e, n_used, xs, w_gate_up, w_gate_up,
      b_gate_up.reshape(E, 1, two_de), b_gate_up.reshape(E, 1, two_de))


def _expert_down_kernel(be_ref, nu_ref, a_ref, w_ref, bias_ref, o_ref, w_bf):
    b = pl.program_id(1)

    @pl.when(b >= nu_ref[0])
    def _():
        o_ref[...] = jnp.zeros_like(o_ref)

    @pl.when(b < nu_ref[0])
    def _():
        @pl.when(_expert_changed(be_ref, b))
        def _():
            w_bf[...] = w_ref[...].astype(BF16)

        o_ref[...] = jnp.dot(a_ref[...], w_bf[...], preferred_element_type=F32) + bias_ref[...]


def _expert_down(block_e, n_used, act, w_down, b_down):
    R, DE = act.shape
    E, _, D = w_down.shape
    tn = min(1024, D)
    nj = D // tn
    nb = R // MOE_BLOCK

    def blk(b, nu):
        return jnp.minimum(b, nu[0] - 1)

    return pl.pallas_call(
        _expert_down_kernel,
        grid_spec=pltpu.PrefetchScalarGridSpec(
            num_scalar_prefetch=2,
            grid=(nj, nb),
            in_specs=[
                pl.BlockSpec((MOE_BLOCK, DE), lambda j, b, be, nu: (blk(b, nu), 0)),
                pl.BlockSpec((None, DE, tn), lambda j, b, be, nu: (be[blk(b, nu)], 0, j)),
                pl.BlockSpec((None, 1, tn), lambda j, b, be, nu: (be[blk(b, nu)], 0, j)),
            ],
            out_specs=pl.BlockSpec((MOE_BLOCK, tn), lambda j, b, be, nu: (b, j)),
            scratch_shapes=[pltpu.VMEM((DE, tn), BF16)]),
        out_shape=jax.ShapeDtypeStruct((R, D), F32),
        compiler_params=_params(2),
        name="expert_down",
    )(block_e, n_used, act, w_down, b_down.reshape(E, 1, D))


def _combine_kernel(slots_ref, out_hbm, x_ref, gate_ref, w_ref, o_ref, buf, sem, *, tc, n_tiles):
    i = pl.program_id(0)

    def issue(tile, which):
        def body(t, carry):
            for j in range(TOP_K):
                s = slots_ref[(tile * tc + t) * TOP_K + j]
                pltpu.make_async_copy(out_hbm.at[pl.ds(s, 1)],
                                      buf.at[which, pl.ds(j * tc + t, 1)],
                                      sem.at[which]).start()
            return carry
        lax.fori_loop(0, tc, body, 0)

    @pl.when(i == 0)
    def _():
        issue(0, 0)

    @pl.when(i + 1 < n_tiles)
    def _():
        issue(i + 1, (i + 1) % 2)

    cur = i % 2
    pltpu.make_async_copy(out_hbm.at[pl.ds(0, TOP_K * tc)], buf.at[cur], sem.at[cur]).wait()
    gates = gate_ref[...]
    y = x_ref[...]
    for j in range(TOP_K):
        y = y + gates[:, j:j + 1] * buf[cur, j * tc:(j + 1) * tc, :]
    o_ref[...] = y * lax.rsqrt(jnp.mean(y * y, axis=-1, keepdims=True) + NORM_EPS) * w_ref[...]


def _combine(slots_flat, out_rows, x1, gates, w):
    T, D = x1.shape
    tc = min(128, T // 2)
    n_tiles = T // tc
    return pl.pallas_call(
        functools.partial(_combine_kernel, tc=tc, n_tiles=n_tiles),
        grid_spec=pltpu.PrefetchScalarGridSpec(
            num_scalar_prefetch=1,
            grid=(n_tiles,),
            in_specs=[pl.BlockSpec(memory_space=pl.ANY),
                      pl.BlockSpec((tc, D), lambda i, s: (i, 0)),
                      pl.BlockSpec((tc, LANES), lambda i, s: (i, 0)),
                      pl.BlockSpec((1, D), lambda i, s: (0, 0))],
            out_specs=pl.BlockSpec((tc, D), lambda i, s: (i, 0)),
            scratch_shapes=[pltpu.VMEM((2, TOP_K * tc, D), F32),
                            pltpu.SemaphoreType.DMA((2,))]),
        out_shape=jax.ShapeDtypeStruct((T, D), F32),
        compiler_params=_params(1),
        name="combine",
    )(slots_flat, out_rows, x1, gates, w)


def _pad_lanes(a):
    return jnp.pad(a, ((0, 0), (0, LANES - a.shape[1])))


def kernel(x, attn_norm_w, w_in, gdn_conv_w, gdn_a_log, gdn_dt_bias, gdn_norm_w, sc_conv_w, w_out,
           ffn_norm_w, router_w, router_b, w_gate_up, b_gate_up, w_down, b_down, final_norm_w):
    Bsz, S, D = x.shape
    assert Bsz == 1, "sequence tiles carry conv / recurrent state, one sequence per call"
    depth = w_in.shape[0]
    assert depth == 1, "the final rmsnorm is fused into the combine step of the only layer"
    T = Bsz * S
    H = gdn_a_log.shape[1]
    gdn_w = H * HEAD_DIM
    sc_w = sc_conv_w.shape[2]
    E = router_w.shape[2]
    assert gdn_conv_w.shape[2] == 3 * gdn_w and 2 * H <= LANES and E <= LANES
    n_assign = T * TOP_K
    n_blocks = -(-(n_assign + E * (MOE_BLOCK - 1)) // MOE_BLOCK)

    xt = x.reshape(T, D)
    for l in range(depth):
        w_qkvz = w_in[l, :, :4 * gdn_w].astype(BF16)
        w_ba = _pad_lanes(w_in[l, :, 4 * gdn_w:4 * gdn_w + 2 * H]).astype(BF16)
        w_sc = w_in[l, :, 4 * gdn_w + 2 * H:].astype(BF16)
        h, ba = _norm_in(xt, attn_norm_w[l][None, :], w_ba)
        qkvz = _matmul(h, w_qkvz, "proj_qkvz")
        proj_sc = _matmul(h, w_sc, "proj_sc")
        par = jnp.zeros((SUBLANES, LANES), F32)
        par = par.at[0, :H].set(gdn_a_log[l]).at[1, :H].set(gdn_dt_bias[l])
        y_gdn = _gdn(qkvz, ba, gdn_conv_w[l], par, gdn_norm_w[l][None, :], H)
        y_sc = _shortconv(proj_sc, sc_conv_w[l])
        x1 = _out_proj(xt, y_gdn, y_sc, w_out[l].astype(BF16))

        h2, top_idx, gates, rank, counts = _router(
            x1, ffn_norm_w[l][None, :], _pad_lanes(router_w[l]), _pad_lanes(router_b[l][None, :]), E)
        counts = counts[0, :E]
        padded = (counts + MOE_BLOCK - 1) // MOE_BLOCK * MOE_BLOCK
        pad_end = jnp.cumsum(padded)
        pad_start = pad_end - padded
        slots = (pad_start[top_idx[:, :TOP_K]] + rank[:, :TOP_K]).reshape(n_assign).astype(jnp.int32)
        blk_start = jnp.arange(n_blocks, dtype=jnp.int32) * MOE_BLOCK
        block_e = jnp.minimum(jnp.searchsorted(pad_end, blk_start, side='right'), E - 1).astype(jnp.int32)
        n_used = (pad_end[-1:] // MOE_BLOCK).astype(jnp.int32)

        xs = _scatter(slots, h2, n_blocks * MOE_BLOCK)
        act = _expert_up(block_e, n_used, xs, w_gate_up[l], b_gate_up[l])
        out_rows = _expert_down(block_e, n_used, act, w_down[l], b_down[l])
        xt = _combine(slots, out_rows, x1, gates, final_norm_w[None, :])
    return xt.reshape(Bsz, S, D)
```
